```python
import math
import jax, jax.numpy as jnp
from jax import lax
import numpy as np

D_MODEL = 2048
BATCH = 2
SEQ = 8192
DEPTH = 1

MIX_WIDTH = D_MODEL
CONV_WIDTH = D_MODEL // 2
CONV_K = 3
ATTN_WIDTH = MIX_WIDTH - CONV_WIDTH
DIFF_HEAD_DIM = 64
N_DIFF_HEADS = ATTN_WIDTH // (2 * DIFF_HEAD_DIM)
DIFF_V_DIM = 2 * DIFF_HEAD_DIM
Q_BLOCK = 128
ROPE_THETA = 10000.0
NORM_EPS = 1e-6
SUBLN_EPS = 1e-5
N_KEYS = 128
N_EXPERTS = N_KEYS * N_KEYS
PEER_HEADS = 8
PEER_KEY_DIM = 256
PEER_HALF = PEER_KEY_DIM // 2
PEER_TOPK = 16
PEER_BLOCK = 128
N_ADA = 6
IN_COLS = 3 * CONV_WIDTH + 3 * ATTN_WIDTH

kernel_name = "hymba_shortconv_diffattn_peer_adaln"


def rms_norm(x, g, eps=NORM_EPS):
    xf = x.astype(jnp.float32)
    y = xf * lax.rsqrt(jnp.mean(xf * xf, axis=-1, keepdims=True) + eps)
    return (y * g.astype(jnp.float32)).astype(x.dtype)


def modulate(h, shift, scale):
    return h * (1.0 + scale[:, None, :]) + shift[:, None, :]


def rope_tables(positions, dim):
    inv_freq = ROPE_THETA ** (-jnp.arange(0, dim, 2, dtype=jnp.float32) / dim)
    ang = positions.astype(jnp.float32)[..., None] * inv_freq
    ang = jnp.concatenate([ang, ang], axis=-1)
    return jnp.cos(ang)[:, :, None, None, :], jnp.sin(ang)[:, :, None, None, :]


def apply_rope(x, cos, sin):
    half = x.shape[-1] // 2
    x1, x2 = x[..., :half], x[..., half:]
    rot = jnp.concatenate([-x2, x1], axis=-1)
    return x.astype(jnp.float32) * cos + rot.astype(jnp.float32) * sin


def causal_short_conv(z, w):
    S = z.shape[1]
    zp = jnp.pad(z, ((0, 0), (CONV_K - 1, 0), (0, 0)))
    return sum(w[j] * lax.dynamic_slice_in_dim(zp, j, S, axis=1) for j in range(CONV_K))


def diff_attention(hq, hk, hv, positions, q_norm_g, k_norm_g, lam, subln_g, lam_init):
    B, S, _ = hq.shape
    H, Dh, E = N_DIFF_HEADS, DIFF_HEAD_DIM, DIFF_V_DIM
    q = rms_norm(hq.reshape(B, S, H, 2, Dh), q_norm_g)
    k = rms_norm(hk.reshape(B, S, H, 2, Dh), k_norm_g)
    cos, sin = rope_tables(positions, Dh)
    q32 = apply_rope(q, cos, sin) * (Dh ** -0.5)
    k32 = apply_rope(k, cos, sin)
    v32 = hv.reshape(B, S, H, E).astype(jnp.float32)
    kpos = jnp.arange(S)
    n_blocks = S // Q_BLOCK

    def block(i):
        qb = lax.dynamic_slice_in_dim(q32, i * Q_BLOCK, Q_BLOCK, axis=1)
        s = jnp.einsum('bqhmd,bkhmd->bhmqk', qb, k32)
        qpos = i * Q_BLOCK + jnp.arange(Q_BLOCK)
        mask = kpos[None, :] <= qpos[:, None]
        p = jax.nn.softmax(jnp.where(mask, s, -jnp.inf), axis=-1)
        a = p[:, :, 0] - lam * p[:, :, 1]
        return jnp.einsum('bhqk,bkhe->bqhe', a, v32)

    o = lax.map(block, jnp.arange(n_blocks))
    o = jnp.moveaxis(o, 0, 1).reshape(B, S, H, E)
    o = rms_norm(o, subln_g, SUBLN_EPS) * (1.0 - lam_init)
    return o.reshape(B, S, H * E).astype(hq.dtype)


def peer_ffn(h, w_q, keys1, keys2, expert_u, expert_v):
    B, S, D = h.shape
    T = B * S
    hf = h.reshape(T, D)
    q = (hf @ w_q).reshape(T, PEER_HEADS, 2, PEER_HALF)
    s1 = jnp.einsum('thd,nd->thn', q[:, :, 0], keys1)
    s2 = jnp.einsum('thd,nd->thn', q[:, :, 1], keys2)
    v1, i1 = lax.top_k(s1, PEER_TOPK)
    v2, i2 = lax.top_k(s2, PEER_TOPK)
    cand = (v1[..., :, None] + v2[..., None, :]).reshape(T, PEER_HEADS, PEER_TOPK * PEER_TOPK)
    cand_idx = (i1[..., :, None] * N_KEYS + i2[..., None, :]).reshape(T, PEER_HEADS, PEER_TOPK * PEER_TOPK)
    top_s, top_p = lax.top_k(cand, PEER_TOPK)
    idx = jnp.take_along_axis(cand_idx, top_p, axis=-1)
    g = jax.nn.softmax(top_s.astype(jnp.float32), axis=-1)

    def apply(args):
        xb, ib, gb = args
        u = jnp.take(expert_u, ib, axis=0)
        a = jnp.einsum('thkd,td->thk', u, xb)
        w = (jax.nn.gelu(a.astype(jnp.float32), approximate=False) * gb).astype(xb.dtype)
        vv = jnp.take(expert_v, ib, axis=0)
        return jnp.einsum('thk,thkd->td', w, vv)

    nb = T // PEER_BLOCK
    out = lax.map(apply, (hf.reshape(nb, PEER_BLOCK, D),
                          idx.reshape(nb, PEER_BLOCK, PEER_HEADS, PEER_TOPK),
                          g.reshape(nb, PEER_BLOCK, PEER_HEADS, PEER_TOPK)))
    return out.reshape(B, S, D)


def setup_inputs(seed: int = 0) -> dict:
    key = jax.random.key(seed)
    ks = jax.random.split(key, 24)
    f32 = jnp.float32
    nrm = lambda k, shape, std: jax.random.normal(k, shape, f32) * std
    gain = lambda k, shape: 1.0 + 0.02 * jax.random.normal(k, shape, f32)
    L, D = DEPTH, D_MODEL
    return {
        "x": nrm(ks[0], (BATCH, SEQ, D), 1.0),
        "c": nrm(ks[1], (BATCH, D), 1.0),
        "positions": jnp.broadcast_to(jnp.arange(SEQ, dtype=jnp.int32), (BATCH, SEQ)),
        "w_ada": nrm(ks[2], (L, D, N_ADA * D), D ** -0.5),
        "b_ada": nrm(ks[3], (L, N_ADA * D), 0.02),
        "norm1_g": gain(ks[4], (L, D)),
        "w_in": nrm(ks[5], (L, D, IN_COLS), D ** -0.5),
        "conv_w": nrm(ks[6], (L, CONV_K, CONV_WIDTH), CONV_K ** -0.5),
        "q_norm_g": gain(ks[7], (L, DIFF_HEAD_DIM)),
        "k_norm_g": gain(ks[8], (L, DIFF_HEAD_DIM)),
        "lambda_q1": nrm(ks[9], (L, DIFF_HEAD_DIM), 0.1),
        "lambda_k1": nrm(ks[10], (L, DIFF_HEAD_DIM), 0.1),
        "lambda_q2": nrm(ks[11], (L, DIFF_HEAD_DIM), 0.1),
        "lambda_k2": nrm(ks[12], (L, DIFF_HEAD_DIM), 0.1),
        "subln_g": gain(ks[13], (L, DIFF_V_DIM)),
        "w_out": nrm(ks[14], (L, MIX_WIDTH, D), MIX_WIDTH ** -0.5),
        "norm2_g": gain(ks[15], (L, D)),
        "w_peer_q": nrm(ks[16], (L, D, PEER_HEADS * PEER_KEY_DIM), D ** -0.5),
        "sub_keys1": nrm(ks[17], (L, N_KEYS, PEER_HALF), PEER_HALF ** -0.5),
        "sub_keys2": nrm(ks[18], (L, N_KEYS, PEER_HALF), PEER_HALF ** -0.5),
        "expert_u": nrm(ks[19], (L, N_EXPERTS, D), D ** -0.5),
        "expert_v": nrm(ks[20], (L, N_EXPERTS, D), PEER_HEADS ** -0.5),
    }


def reference(x, c, positions, w_ada, b_ada, norm1_g, w_in, conv_w, q_norm_g, k_norm_g,
              lambda_q1, lambda_k1, lambda_q2, lambda_k2, subln_g, w_out, norm2_g,
              w_peer_q, sub_keys1, sub_keys2, expert_u, expert_v):
    split_at = [CONV_WIDTH, 2 * CONV_WIDTH, 3 * CONV_WIDTH,
                3 * CONV_WIDTH + ATTN_WIDTH, 3 * CONV_WIDTH + 2 * ATTN_WIDTH]
    for l in range(DEPTH):
        lam_init = 0.8 - 0.6 * math.exp(-0.3 * l)
        mod = jax.nn.silu(c) @ w_ada[l] + b_ada[l]
        shift1, scale1, gate1, shift2, scale2, gate2 = jnp.split(mod, N_ADA, axis=-1)

        h = modulate(rms_norm(x, norm1_g[l]), shift1, scale1)
        proj = h @ w_in[l]
        cb, cc, ch, aq, ak, av = jnp.split(proj, split_at, axis=-1)
        y_conv = cb * causal_short_conv(cc * ch, conv_w[l])
        lam = (jnp.exp(jnp.sum(lambda_q1[l].astype(jnp.float32) * lambda_k1[l].astype(jnp.float32)))
               - jnp.exp(jnp.sum(lambda_q2[l].astype(jnp.float32) * lambda_k2[l].astype(jnp.float32)))
               + lam_init)
        y_attn = diff_attention(aq, ak, av, positions, q_norm_g[l], k_norm_g[l],
                                lam, subln_g[l], lam_init)
        mix = jnp.concatenate([y_conv, y_attn], axis=-1) @ w_out[l]
        x = x + gate1[:, None, :] * mix

        h2 = modulate(rms_norm(x, norm2_g[l]), shift2, scale2)
        ffn = peer_ffn(h2, w_peer_q[l], sub_keys1[l], sub_keys2[l], expert_u[l], expert_v[l])
        x = x + gate2[:, None, :] * ffn
    return x
```

```python
import functools
import math

import jax
import jax.numpy as jnp
from jax import lax
from jax.experimental import pallas as pl
from jax.experimental.pallas import tpu as pltpu

F32 = jnp.float32
BF16 = jnp.bfloat16

NORM_EPS = 1e-6
SUBLN_EPS = 1e-5
ROPE_THETA = 10000.0
HEAD_DIM = 64
N_HEADS = 8
V_DIM = 128
CONV_WIDTH = 1024
ATTN_WIDTH = 1024
N_KEYS = 128
PEER_HEADS = 8
PEER_TOPK = 16
LOG2E = 1.4426950408889634

V7X_VMEM_LIMIT_BYTES = 56 * 1024 * 1024
LANES = 128

NT_DIMS = (((1,), (1,)), ((), ()))


def _params(*sem):
    return pltpu.CompilerParams(dimension_semantics=sem, vmem_limit_bytes=V7X_VMEM_LIMIT_BYTES)


def _ada_kernel(c_ref, w_ref, b_ref, o_ref):
    c = c_ref[...]
    sc = c / (1.0 + jnp.exp(-c))
    w = w_ref[...]
    sc_hi = sc.astype(BF16)
    sc_lo = (sc - sc_hi.astype(F32)).astype(BF16)
    w_hi = w.astype(BF16)
    w_lo = (w - w_hi.astype(F32)).astype(BF16)
    acc = jnp.dot(sc_hi, w_hi, preferred_element_type=F32)
    acc += jnp.dot(sc_hi, w_lo, preferred_element_type=F32)
    acc += jnp.dot(sc_lo, w_hi, preferred_element_type=F32)
    o_ref[...] = acc + b_ref[...]


def ada_mod(c_pad, w_ada, b_ada, *, tn=1024):
    rows, d = c_pad.shape
    n = w_ada.shape[1]
    return pl.pallas_call(
        _ada_kernel,
        grid=(n // tn,),
        in_specs=[pl.BlockSpec((rows, d), lambda j: (0, 0)),
                  pl.BlockSpec((d, tn), lambda j: (0, j)),
                  pl.BlockSpec((1, tn), lambda j: (0, j))],
        out_specs=pl.BlockSpec((rows, tn), lambda j: (0, j)),
        out_shape=jax.ShapeDtypeStruct((rows, n), F32),
        compiler_params=_params("arbitrary"),
        name="ada_mod",
    )(c_pad, w_ada, b_ada)


def _modulated_norm(x, g, shift, scale):
    ms = jnp.mean(x * x, axis=-1, keepdims=True)
    return (x * lax.rsqrt(ms + NORM_EPS) * g) * (1.0 + scale) + shift


def _in_proj_kernel(x_ref, g_ref, sh_ref, sc_ref, w_ref, o_ref, h_ref):
    @pl.when(pl.program_id(1) == 0)
    def _():
        h_ref[...] = _modulated_norm(x_ref[...], g_ref[...], sh_ref[0], sc_ref[0]).astype(BF16)

    o_ref[...] = jnp.dot(h_ref[...], w_ref[...], preferred_element_type=F32).astype(o_ref.dtype)


def in_proj(x2, g, shift, scale, w_bf, *, seq, tm=1024, tn=1536):
    t, d = x2.shape
    n = w_bf.shape[1]
    per_batch = seq // tm
    return pl.pallas_call(
        _in_proj_kernel,
        grid=(t // tm, n // tn),
        in_specs=[pl.BlockSpec((tm, d), lambda i, j: (i, 0)),
                  pl.BlockSpec((1, d), lambda i, j: (0, 0)),
                  pl.BlockSpec((1, 1, d), lambda i, j: (i // per_batch, 0, 0)),
                  pl.BlockSpec((1, 1, d), lambda i, j: (i // per_batch, 0, 0)),
                  pl.BlockSpec((d, tn), lambda i, j: (0, j))],
        out_specs=pl.BlockSpec((tm, tn), lambda i, j: (i, j)),
        out_shape=jax.ShapeDtypeStruct((t, n), BF16),
        scratch_shapes=[pltpu.VMEM((tm, d), BF16)],
        compiler_params=_params("arbitrary", "arbitrary"),
        name="in_proj",
    )(x2, g, shift, scale, w_bf)


HALO_ROWS = 16


def _conv_kernel(b_ref, c_ref, h_ref, cp_ref, hp_ref, w_ref, o_ref, *, blocks_per_seq):
    i = pl.program_id(0)
    z = c_ref[...].astype(F32) * h_ref[...].astype(F32)
    zp = cp_ref[...].astype(F32) * hp_ref[...].astype(F32)
    zp = jnp.where(i % blocks_per_seq == 0, 0.0, zp)
    row = lax.broadcasted_iota(jnp.int32, z.shape, 0)
    last = zp[HALO_ROWS - 1:HALO_ROWS]
    last2 = zp[HALO_ROWS - 2:HALO_ROWS - 1]
    z1 = jnp.where(row == 0, last, pltpu.roll(z, 1, 0))
    z2 = jnp.where(row == 0, last2, jnp.where(row == 1, last, pltpu.roll(z, 2, 0)))
    w = w_ref[...]
    y = b_ref[...].astype(F32) * (w[0:1] * z2 + w[1:2] * z1 + w[2:3] * z)
    o_ref[...] = y.astype(o_ref.dtype)


def short_conv(proj, conv_w, *, seq, tc=512):
    t = proj.shape[0]
    cw = CONV_WIDTH
    halo_blocks = tc // HALO_ROWS

    def prev(col):
        return lambda i: (jnp.maximum(i * halo_blocks - 1, 0), col)

    return pl.pallas_call(
        functools.partial(_conv_kernel, blocks_per_seq=seq // tc),
        grid=(t // tc,),
        in_specs=[pl.BlockSpec((tc, cw), lambda i: (i, 0)),
                  pl.BlockSpec((tc, cw), lambda i: (i, 1)),
                  pl.BlockSpec((tc, cw), lambda i: (i, 2)),
                  pl.BlockSpec((HALO_ROWS, cw), prev(1)),
                  pl.BlockSpec((HALO_ROWS, cw), prev(2)),
                  pl.BlockSpec((3, cw), lambda i: (0, 0))],
        out_specs=pl.BlockSpec((tc, cw), lambda i: (i, 0)),
        out_shape=jax.ShapeDtypeStruct((t, cw), BF16),
        compiler_params=_params("arbitrary"),
        name="short_conv",
    )(proj, proj, proj, proj, proj, conv_w)


def _qk_prep_kernel(q_ref, k_ref, v_ref, pos_ref, freq_ref, qg_ref, kg_ref, qo_ref, ko_ref, vo_ref):
    tr = q_ref.shape[0]
    ang = pos_ref[...] * freq_ref[...]
    cos = jnp.cos(ang)
    sin = jnp.sin(ang)
    lane = lax.broadcasted_iota(jnp.int32, (tr, LANES), 1)
    low_half = lane < HEAD_DIM
    first = (lane & (HEAD_DIM // 2)) == 0
    sin_signed = jnp.where(first, -sin, sin)

    def norm_rope(x, g, scale):
        ss = x * x
        tot = jnp.sum(ss, axis=-1, keepdims=True)
        lo = jnp.sum(jnp.where(low_half, ss, 0.0), axis=-1, keepdims=True)
        ms = jnp.where(low_half, lo, tot - lo) * (1.0 / HEAD_DIM)
        xn = x * lax.rsqrt(ms + NORM_EPS) * g
        rot = jnp.where(first, pltpu.roll(xn, LANES - HEAD_DIM // 2, 1), pltpu.roll(xn, HEAD_DIM // 2, 1))
        return (xn * cos + rot * sin_signed) * scale

    q_scale = HEAD_DIM ** -0.5 * LOG2E
    ones_col = jnp.where(lane == 0, 1.0, 0.0).astype(BF16)
    for h in range(N_HEADS):
        sl = slice(h * LANES, (h + 1) * LANES)
        qo_ref[:, sl] = norm_rope(q_ref[:, sl].astype(F32), qg_ref[...], q_scale).astype(BF16)
        ko_ref[:, sl] = norm_rope(k_ref[:, sl].astype(F32), kg_ref[...], 1.0).astype(BF16)
        vo_ref[:, 2 * h * LANES:(2 * h + 1) * LANES] = v_ref[:, sl]
        vo_ref[:, (2 * h + 1) * LANES:(2 * h + 2) * LANES] = ones_col


def qk_prep(proj, pos_col, freq_row, qg_row, kg_row, *, tr=512):
    t = proj.shape[0]
    aw = ATTN_WIDTH
    base = 3 * CONV_WIDTH // aw
    return pl.pallas_call(
        _qk_prep_kernel,
        grid=(t // tr,),
        in_specs=[pl.BlockSpec((tr, aw), lambda i: (i, base)),
                  pl.BlockSpec((tr, aw), lambda i: (i, base + 1)),
                  pl.BlockSpec((tr, aw), lambda i: (i, base + 2)),
                  pl.BlockSpec((tr, 1), lambda i: (i, 0)),
                  pl.BlockSpec((1, LANES), lambda i: (0, 0)),
                  pl.BlockSpec((1, LANES), lambda i: (0, 0)),
                  pl.BlockSpec((1, LANES), lambda i: (0, 0))],
        out_specs=[pl.BlockSpec((tr, aw), lambda i: (i, 0)),
                   pl.BlockSpec((tr, aw), lambda i: (i, 0)),
                   pl.BlockSpec((tr, 2 * aw), lambda i: (i, 0))],
        out_shape=[jax.ShapeDtypeStruct((t, aw), BF16),
                   jax.ShapeDtypeStruct((t, aw), BF16),
                   jax.ShapeDtypeStruct((t, 2 * aw), BF16)],
        compiler_params=_params("arbitrary"),
        name="qk_prep",
    )(proj, proj, proj, pos_col, freq_row, qg_row, kg_row)


def _diff_attn_kernel(q_ref, k_ref, v_ref, lq1_ref, lk1_ref, lq2_ref, lk2_ref, sg_ref, o_ref,
                      q12_ref, m_ref, acc_ref, *, tq, lam_init):
    qi = pl.program_id(2)
    q = q_ref[...]
    lane = lax.broadcasted_iota(jnp.int32, q.shape, 1)
    zero = jnp.zeros_like(q)
    q12_ref[0:tq, :] = jnp.where(lane < HEAD_DIM, q, zero)
    q12_ref[tq:2 * tq, :] = jnp.where(lane < HEAD_DIM, zero, q)
    m_ref[...] = jnp.full(m_ref.shape, -jnp.inf, F32)
    acc_ref[...] = jnp.zeros(acc_ref.shape, F32)

    def step(j, masked):
        start = pl.multiple_of(j * tq, tq)
        k = k_ref[pl.ds(start, tq), :]
        v = v_ref[pl.ds(start, tq), :]
        s = lax.dot_general(q12_ref[...], k, NT_DIMS, preferred_element_type=F32)
        if masked:
            r = lax.broadcasted_iota(jnp.int32, s.shape, 0) % tq
            cidx = lax.broadcasted_iota(jnp.int32, s.shape, 1)
            s = jnp.where(cidx <= r, s, -jnp.inf)
        m_prev = m_ref[...]
        m_new = jnp.maximum(m_prev, jnp.max(s, axis=-1, keepdims=True))
        alpha = jnp.exp2(m_prev - m_new)
        p = jnp.exp2(s - m_new)
        acc_ref[...] = alpha * acc_ref[...] + jnp.dot(p.astype(BF16), v, preferred_element_type=F32)
        m_ref[...] = m_new

    def body(j, carry):
        step(j, False)
        return carry

    lax.fori_loop(0, qi, body, 0)
    step(qi, True)

    acc = acc_ref[...]
    o12 = acc[:, :V_DIM] / acc[:, V_DIM:V_DIM + 1]
    lam = (jnp.exp(jnp.sum(lq1_ref[...] * lk1_ref[...], axis=-1, keepdims=True))
           - jnp.exp(jnp.sum(lq2_ref[...] * lk2_ref[...], axis=-1, keepdims=True)) + lam_init)
    o = o12[0:tq] - lam * o12[tq:2 * tq]
    ms = jnp.mean(o * o, axis=-1, keepdims=True)
    o = o * lax.rsqrt(ms + SUBLN_EPS) * sg_ref[...] * (1.0 - lam_init)
    o_ref[...] = o.astype(o_ref.dtype)


def diff_attn(qh, kh, vp, lq1, lk1, lq2, lk2, subln_g, *, batch, seq, lam_init, tq=512):
    t = qh.shape[0]
    nq = seq // tq
    small = pl.BlockSpec((1, HEAD_DIM), lambda b, h, i: (0, 0))
    return pl.pallas_call(
        functools.partial(_diff_attn_kernel, tq=tq, lam_init=lam_init),
        grid=(batch, N_HEADS, nq),
        in_specs=[pl.BlockSpec((tq, LANES), lambda b, h, i: (b * nq + i, h)),
                  pl.BlockSpec((seq, LANES), lambda b, h, i: (b, h)),
                  pl.BlockSpec((seq, 2 * LANES), lambda b, h, i: (b, h)),
                  small, small, small, small,
                  pl.BlockSpec((1, V_DIM), lambda b, h, i: (0, 0))],
        out_specs=pl.BlockSpec((tq, V_DIM), lambda b, h, i: (b * nq + i, h)),
        out_shape=jax.ShapeDtypeStruct((t, ATTN_WIDTH), BF16),
        scratch_shapes=[pltpu.VMEM((2 * tq, LANES), BF16),
                        pltpu.VMEM((2 * tq, 1), F32),
                        pltpu.VMEM((2 * tq, 2 * LANES), F32)],
        compiler_params=_params("arbitrary", "arbitrary", "arbitrary"),
        name="diff_attn",
    )(qh, kh, vp, lq1, lk1, lq2, lk2, subln_g)


def _out_proj_kernel(yc_ref, ya_ref, wc_ref, wa_ref, x_ref, g1_ref, g_ref, sh_ref, sc_ref, x1_ref, h2_ref):
    mix = jnp.dot(yc_ref[...], wc_ref[...], preferred_element_type=F32)
    mix += jnp.dot(ya_ref[...], wa_ref[...], preferred_element_type=F32)
    x1 = x_ref[...] + g1_ref[0] * mix
    x1_ref[...] = x1
    h2_ref[...] = _modulated_norm(x1, g_ref[...], sh_ref[0], sc_ref[0]).astype(BF16)


def out_proj(yc, ya, w_bf, x2, gate1, g, shift, scale, *, seq, tm=512):
    t, d = x2.shape
    cw = yc.shape[1]
    per_batch = seq // tm
    mod_spec = pl.BlockSpec((1, 1, d), lambda i: (i // per_batch, 0, 0))
    return pl.pallas_call(
        _out_proj_kernel,
        grid=(t // tm,),
        in_specs=[pl.BlockSpec((tm, cw), lambda i: (i, 0)),
                  pl.BlockSpec((tm, cw), lambda i: (i, 0)),
                  pl.BlockSpec((cw, d), lambda i: (0, 0)),
                  pl.BlockSpec((cw, d), lambda i: (1, 0)),
                  pl.BlockSpec((tm, d), lambda i: (i, 0)),
                  mod_spec,
                  pl.BlockSpec((1, d), lambda i: (0, 0)),
                  mod_spec, mod_spec],
        out_specs=[pl.BlockSpec((tm, d), lambda i: (i, 0)),
                   pl.BlockSpec((tm, d), lambda i: (i, 0))],
        out_shape=[jax.ShapeDtypeStruct((t, d), F32),
                   jax.ShapeDtypeStruct((t, d), BF16)],
        compiler_params=_params("arbitrary"),
        name="out_proj",
    )(yc, ya, w_bf, w_bf, x2, gate1, g, shift, scale)


SUB = 8


def _peer_select_kernel(h2_ref, wq_ref, keys_ref, s1_ref, s2_ref, e2_ref, r1_ref, tau_ref,
                        qt_ref, s_ref, top_ref, cnt_ref, *, ts):
    qt_ref[...] = lax.dot_general(wq_ref[...], h2_ref[...], NT_DIMS,
                                  preferred_element_type=F32).astype(BF16)
    n_groups = ts // LANES
    neg_inf = -jnp.inf

    def scores_and_top(hc, carry):
        row0 = pl.multiple_of(hc * N_KEYS, N_KEYS)
        s = jnp.dot(keys_ref[hc % 2], qt_ref[pl.ds(row0, N_KEYS), :], preferred_element_type=F32)
        s_ref[hc] = s
        for g in range(n_groups):
            lanes = slice(g * LANES, (g + 1) * LANES)
            x = s[:, lanes]
            for k in range(PEER_TOPK):
                m = jnp.max(x, axis=0, keepdims=True)
                eq = x == m
                top_ref[hc, k:k + 1, lanes] = m
                cnt_ref[hc, k:k + 1, lanes] = jnp.sum(jnp.where(eq, 1.0, 0.0), axis=0, keepdims=True)
                x = jnp.where(eq, neg_inf, x)
        return carry

    lax.fori_loop(0, 2 * PEER_HEADS, scores_and_top, 0)

    def threshold(h, carry):
        for g in range(n_groups):
            lanes = slice(g * LANES, (g + 1) * LANES)
            a = top_ref[2 * h, :, lanes]
            ma = cnt_ref[2 * h, :, lanes]
            b = top_ref[2 * h + 1, :, lanes]
            mb = cnt_ref[2 * h + 1, :, lanes]
            cand = [a[k:k + 1] + b[0:SUB] for k in range(SUB)]
            mult = [ma[k:k + 1] * mb[0:SUB] for k in range(SUB)]
            cand += [a[0:1] + b[SUB:2 * SUB], a[SUB:2 * SUB] + b[0:1]]
            mult += [ma[0:1] * mb[SUB:2 * SUB], ma[SUB:2 * SUB] * mb[0:1]]
            c0 = jnp.concatenate(cand, axis=0)
            mu = jnp.concatenate(mult, axis=0)
            c = c0
            run = jnp.zeros((1, LANES), F32)
            tau = jnp.zeros((1, LANES), F32)
            for _ in range(PEER_TOPK):
                m = jnp.max(c, axis=0, keepdims=True)
                eq = c == m
                tau = jnp.where(run < PEER_TOPK, m, tau)
                run = run + jnp.sum(jnp.where(eq, mu, 0.0), axis=0, keepdims=True)
                c = jnp.where(eq, neg_inf, c)
            top = a[0:1] + b[0:1]
            z = jnp.sum(jnp.where(c0 >= tau, mu * jnp.exp(c0 - top), 0.0), axis=0, keepdims=True)
            s1 = s_ref[2 * h, :, lanes]
            s2 = s_ref[2 * h + 1, :, lanes]
            tau_ref[h, :, lanes] = tau
            s1_ref[h, :, lanes] = s1
            s2_ref[h, :, lanes] = s2
            e2_ref[h, :, lanes] = jnp.exp(s2 - b[0:1])
            r1_ref[h, :, lanes] = jnp.exp(s1 - a[0:1]) / z
        return carry

    lax.fori_loop(0, PEER_HEADS, threshold, 0)


def peer_select(h2, wq_t, keys, *, ts=256):
    t, d = h2.shape
    nq = wq_t.shape[0]
    hp = PEER_HEADS
    slab = pl.BlockSpec((hp, N_KEYS, ts), lambda i: (0, 0, i))
    slab_shape = jax.ShapeDtypeStruct((hp, N_KEYS, t), F32)
    return pl.pallas_call(
        functools.partial(_peer_select_kernel, ts=ts),
        grid=(t // ts,),
        in_specs=[pl.BlockSpec((ts, d), lambda i: (i, 0)),
                  pl.BlockSpec((nq, d), lambda i: (0, 0)),
                  pl.BlockSpec((2, N_KEYS, N_KEYS), lambda i: (0, 0, 0))],
        out_specs=[slab, slab, slab, slab,
                   pl.BlockSpec((hp, 1, ts), lambda i: (0, 0, i))],
        out_shape=[slab_shape, slab_shape, slab_shape, slab_shape,
                   jax.ShapeDtypeStruct((hp, 1, t), F32)],
        scratch_shapes=[pltpu.VMEM((nq, ts), BF16),
                        pltpu.VMEM((2 * hp, N_KEYS, ts), F32),
                        pltpu.VMEM((2 * hp, PEER_TOPK, ts), F32),
                        pltpu.VMEM((2 * hp, PEER_TOPK, ts), F32)],
        compiler_params=_params("arbitrary"),
        name="peer_select",
    )(h2, wq_t, keys)


ROWS_PER_STEP = SUB
EXPERTS_PER_STEP = ROWS_PER_STEP * N_KEYS


def _peer_ffn_kernel(h2_ref, u_ref, vt_ref, s1_ref, r1_ref, s2_ref, e2_ref, tau_ref, x1_ref, g2_ref, o_ref,
                     acc_ref, a_ref, w_ref):
    c = pl.program_id(1)
    tb = a_ref.shape[1]

    @pl.when(c == 0)
    def _():
        acc_ref[...] = jnp.zeros(acc_ref.shape, F32)

    a_ref[...] = lax.dot_general(u_ref[...], h2_ref[...], NT_DIMS, preferred_element_type=F32)

    for k in range(ROWS_PER_STEP):
        rows = slice(k * N_KEYS, (k + 1) * N_KEYS)
        for g in range(tb // LANES):
            lanes = slice(g * LANES, (g + 1) * LANES)
            gsum = jnp.zeros((N_KEYS, LANES), F32)
            for h in range(PEER_HEADS):
                pair = s2_ref[h, :, lanes] + s1_ref[h, k:k + 1, lanes]
                sel = jnp.where(pair >= tau_ref[h, :, lanes], e2_ref[h, :, lanes], 0.0)
                gsum = gsum + sel * r1_ref[h, k:k + 1, lanes]
            a = a_ref[rows, lanes]
            gelu = 0.5 * a * (1.0 + lax.erf(a * (2.0 ** -0.5)))
            w_ref[rows, lanes] = (gelu * gsum).astype(BF16)

    acc_ref[...] += jnp.dot(vt_ref[...], w_ref[...], preferred_element_type=F32)

    @pl.when(c == pl.num_programs(1) - 1)
    def _():
        o_ref[...] = x1_ref[...] + g2_ref[0] * acc_ref[...].T


def peer_ffn(h2, u_bf, vt_bf, s1, s2, e2, r1, tau, x1, gate2, *, seq, tb=512):
    t, d = h2.shape
    n_exp = u_bf.shape[0]
    hp = PEER_HEADS
    ec = EXPERTS_PER_STEP
    per_batch = seq // tb
    rows = pl.BlockSpec((hp, ROWS_PER_STEP, tb), lambda b, c: (0, c, b))
    slab = pl.BlockSpec((hp, N_KEYS, tb), lambda b, c: (0, 0, b))
    return pl.pallas_call(
        _peer_ffn_kernel,
        grid=(t // tb, n_exp // ec),
        in_specs=[pl.BlockSpec((tb, d), lambda b, c: (b, 0)),
                  pl.BlockSpec((ec, d), lambda b, c: (c, 0)),
                  pl.BlockSpec((d, ec), lambda b, c: (0, c)),
                  rows, rows, slab, slab,
                  pl.BlockSpec((hp, 1, tb), lambda b, c: (0, 0, b)),
                  pl.BlockSpec((tb, d), lambda b, c: (b, 0), pipeline_mode=pl.Buffered(1)),
                  pl.BlockSpec((1, 1, d), lambda b, c: (b // per_batch, 0, 0))],
        out_specs=pl.BlockSpec((tb, d), lambda b, c: (b, 0)),
        out_shape=jax.ShapeDtypeStruct((t, d), F32),
        scratch_shapes=[pltpu.VMEM((d, tb), F32),
                        pltpu.VMEM((ec, tb), F32),
                        pltpu.VMEM((ec, tb), BF16)],
        compiler_params=_params("arbitrary", "arbitrary"),
        name="peer_ffn",
    )(h2, u_bf, vt_bf, s1, r1, s2, e2, tau, x1, gate2)


def _layer(x2, c_pad, pos_col, w_ada, b_ada, norm1_g, w_in, conv_w, q_norm_g, k_norm_g,
           lambda_q1, lambda_k1, lambda_q2, lambda_k2, subln_g, w_out, norm2_g,
           w_peer_q, sub_keys1, sub_keys2, expert_u, expert_v, *, batch, seq, layer_index):
    d = x2.shape[1]
    lam_init = 0.8 - 0.6 * math.exp(-0.3 * layer_index)

    mod = ada_mod(c_pad, w_ada, b_ada.reshape(1, -1))[:batch]
    shift1, scale1, gate1, shift2, scale2, gate2 = [m.reshape(batch, 1, d) for m in jnp.split(mod, 6, axis=-1)]

    proj = in_proj(x2, norm1_g.reshape(1, d), shift1, scale1, w_in.astype(BF16), seq=seq)
    y_conv = short_conv(proj, conv_w, seq=seq)

    inv_freq = ROPE_THETA ** (-jnp.arange(0, HEAD_DIM, 2, dtype=F32) / HEAD_DIM)
    freq_row = jnp.tile(inv_freq, LANES // inv_freq.shape[0]).reshape(1, LANES)
    qg_row = jnp.tile(q_norm_g, LANES // HEAD_DIM).reshape(1, LANES)
    kg_row = jnp.tile(k_norm_g, LANES // HEAD_DIM).reshape(1, LANES)
    qh, kh, vp = qk_prep(proj, pos_col, freq_row, qg_row, kg_row)
    y_attn = diff_attn(qh, kh, vp, lambda_q1.reshape(1, -1), lambda_k1.reshape(1, -1),
                       lambda_q2.reshape(1, -1), lambda_k2.reshape(1, -1), subln_g.reshape(1, -1),
                       batch=batch, seq=seq, lam_init=lam_init)

    x1, h2 = out_proj(y_conv, y_attn, w_out.astype(BF16), x2, gate1, norm2_g.reshape(1, d),
                      shift2, scale2, seq=seq)

    keys = jnp.stack([sub_keys1, sub_keys2]).astype(BF16)
    s1, s2, e2, r1, tau = peer_select(h2, w_peer_q.T.astype(BF16), keys)
    return peer_ffn(h2, expert_u.astype(BF16), expert_v.T.astype(BF16), s1, s2, e2, r1, tau, x1, gate2, seq=seq)


def kernel(x, c, positions, w_ada, b_ada, norm1_g, w_in, conv_w, q_norm_g, k_norm_g, lambda_q1, lambda_k1,
           lambda_q2, lambda_k2, subln_g, w_out, norm2_g, w_peer_q, sub_keys1, sub_keys2, expert_u, expert_v):
    batch, seq, d = x.shape
    depth = w_ada.shape[0]
    x2 = x.reshape(batch * seq, d)
    c_pad = jnp.pad(c, ((0, SUB - batch), (0, 0)))
    pos_col = positions.astype(F32).reshape(batch * seq, 1)
    for l in range(depth):
        x2 = _layer(x2, c_pad, pos_col, w_ada[l], b_ada[l], norm1_g[l], w_in[l], conv_w[l], q_norm_g[l],
                    k_norm_g[l], lambda_q1[l], lambda_k1[l], lambda_q2[l], lambda_k2[l], subln_g[l],
                    w_out[l], norm2_g[l], w_peer_q[l], sub_keys1[l], sub_keys2[l], expert_u[l], expert_v[l],
                    batch=batch, seq=seq, layer_index=l)
    return x2.reshape(batch, seq, d)
```

```python
import functools
import math

import jax
import jax.numpy as jnp
from jax import lax
from jax.experimental import pallas as pl
from jax.experimental.pallas import tpu as pltpu

F32 = jnp.float32
BF16 = jnp.bfloat16

NORM_EPS = 1e-6
SUBLN_EPS = 1e-5
ROPE_THETA = 10000.0
HEAD_DIM = 64
N_HEADS = 8
V_DIM = 128
CONV_WIDTH = 1024
ATTN_WIDTH = 1024
N_KEYS = 128
PEER_HEADS = 8
PEER_TOPK = 16
LOG2E = 1.4426950408889634

V7X_VMEM_LIMIT_BYTES = 56 * 1024 * 1024
LANES = 128

NT_DIMS = (((1,), (1,)), ((), ()))


INTERLEAVE_STREAMS = None


def _params(*sem, flags=None):
    return pltpu.CompilerParams(dimension_semantics=sem, vmem_limit_bytes=V7X_VMEM_LIMIT_BYTES, flags=flags)


def _ada_kernel(c_ref, w_ref, b_ref, o_ref):
    c = c_ref[...]
    sc = c / (1.0 + jnp.exp(-c))
    w = w_ref[...]
    sc_hi = sc.astype(BF16)
    sc_lo = (sc - sc_hi.astype(F32)).astype(BF16)
    w_hi = w.astype(BF16)
    w_lo = (w - w_hi.astype(F32)).astype(BF16)
    acc = jnp.dot(sc_hi, w_hi, preferred_element_type=F32)
    acc += jnp.dot(sc_hi, w_lo, preferred_element_type=F32)
    acc += jnp.dot(sc_lo, w_hi, preferred_element_type=F32)
    o_ref[...] = acc + b_ref[...]


def ada_mod(c_pad, w_ada, b_ada, *, tn=1024):
    rows, d = c_pad.shape
    n = w_ada.shape[1]
    return pl.pallas_call(
        _ada_kernel,
        grid=(n // tn,),
        in_specs=[pl.BlockSpec((rows, d), lambda j: (0, 0)),
                  pl.BlockSpec((d, tn), lambda j: (0, j)),
                  pl.BlockSpec((1, tn), lambda j: (0, j))],
        out_specs=pl.BlockSpec((rows, tn), lambda j: (0, j)),
        out_shape=jax.ShapeDtypeStruct((rows, n), F32),
        compiler_params=_params("arbitrary"),
        name="ada_mod",
    )(c_pad, w_ada, b_ada)


def _modulated_norm(x, g, shift, scale):
    ms = jnp.mean(x * x, axis=-1, keepdims=True)
    return (x * lax.rsqrt(ms + NORM_EPS) * g) * (1.0 + scale) + shift


def _in_proj_kernel(x_ref, g_ref, sh_ref, sc_ref, w_ref, o_ref, h_ref):
    @pl.when(pl.program_id(1) == 0)
    def _():
        h_ref[...] = _modulated_norm(x_ref[...], g_ref[...], sh_ref[0], sc_ref[0]).astype(BF16)

    o_ref[...] = jnp.dot(h_ref[...], w_ref[...], preferred_element_type=F32).astype(o_ref.dtype)


def in_proj(x2, g, shift, scale, w_bf, *, seq, tm=1024, tn=1536):
    t, d = x2.shape
    n = w_bf.shape[1]
    per_batch = seq // tm
    return pl.pallas_call(
        _in_proj_kernel,
        grid=(t // tm, n // tn),
        in_specs=[pl.BlockSpec((tm, d), lambda i, j: (i, 0)),
                  pl.BlockSpec((1, d), lambda i, j: (0, 0)),
                  pl.BlockSpec((1, 1, d), lambda i, j: (i // per_batch, 0, 0)),
                  pl.BlockSpec((1, 1, d), lambda i, j: (i // per_batch, 0, 0)),
                  pl.BlockSpec((d, tn), lambda i, j: (0, j))],
        out_specs=pl.BlockSpec((tm, tn), lambda i, j: (i, j)),
        out_shape=jax.ShapeDtypeStruct((t, n), BF16),
        scratch_shapes=[pltpu.VMEM((tm, d), BF16)],
        compiler_params=_params("arbitrary", "arbitrary"),
        name="in_proj",
    )(x2, g, shift, scale, w_bf)


HALO_ROWS = 16


def _conv_kernel(b_ref, c_ref, h_ref, cp_ref, hp_ref, w_ref, o_ref, *, blocks_per_seq):
    i = pl.program_id(0)
    z = c_ref[...].astype(F32) * h_ref[...].astype(F32)
    zp = cp_ref[...].astype(F32) * hp_ref[...].astype(F32)
    zp = jnp.where(i % blocks_per_seq == 0, 0.0, zp)
    row = lax.broadcasted_iota(jnp.int32, z.shape, 0)
    last = zp[HALO_ROWS - 1:HALO_ROWS]
    last2 = zp[HALO_ROWS - 2:HALO_ROWS - 1]
    z1 = jnp.where(row == 0, last, pltpu.roll(z, 1, 0))
    z2 = jnp.where(row == 0, last2, jnp.where(row == 1, last, pltpu.roll(z, 2, 0)))
    w = w_ref[...]
    y = b_ref[...].astype(F32) * (w[0:1] * z2 + w[1:2] * z1 + w[2:3] * z)
    o_ref[...] = y.astype(o_ref.dtype)


def short_conv(proj, conv_w, *, seq, tc=512):
    t = proj.shape[0]
    cw = CONV_WIDTH
    halo_blocks = tc // HALO_ROWS

    def prev(col):
        return lambda i: (jnp.maximum(i * halo_blocks - 1, 0), col)

    return pl.pallas_call(
        functools.partial(_conv_kernel, blocks_per_seq=seq // tc),
        grid=(t // tc,),
        in_specs=[pl.BlockSpec((tc, cw), lambda i: (i, 0)),
                  pl.BlockSpec((tc, cw), lambda i: (i, 1)),
                  pl.BlockSpec((tc, cw), lambda i: (i, 2)),
                  pl.BlockSpec((HALO_ROWS, cw), prev(1)),
                  pl.BlockSpec((HALO_ROWS, cw), prev(2)),
                  pl.BlockSpec((3, cw), lambda i: (0, 0))],
        out_specs=pl.BlockSpec((tc, cw), lambda i: (i, 0)),
        out_shape=jax.ShapeDtypeStruct((t, cw), BF16),
        compiler_params=_params("arbitrary"),
        name="short_conv",
    )(proj, proj, proj, proj, proj, conv_w)


def _qk_prep_kernel(q_ref, k_ref, v_ref, pos_ref, freq_ref, qg_ref, kg_ref, qo_ref, ko_ref, vo_ref):
    tr = q_ref.shape[0]
    ang = pos_ref[...] * freq_ref[...]
    cos = jnp.cos(ang)
    sin = jnp.sin(ang)
    lane = lax.broadcasted_iota(jnp.int32, (tr, LANES), 1)
    low_half = lane < HEAD_DIM
    first = (lane & (HEAD_DIM // 2)) == 0
    sin_signed = jnp.where(first, -sin, sin)

    def norm_rope(x, g, scale):
        ss = x * x
        tot = jnp.sum(ss, axis=-1, keepdims=True)
        lo = jnp.sum(jnp.where(low_half, ss, 0.0), axis=-1, keepdims=True)
        ms = jnp.where(low_half, lo, tot - lo) * (1.0 / HEAD_DIM)
        xn = x * lax.rsqrt(ms + NORM_EPS) * g
        rot = jnp.where(first, pltpu.roll(xn, LANES - HEAD_DIM // 2, 1), pltpu.roll(xn, HEAD_DIM // 2, 1))
        return (xn * cos + rot * sin_signed) * scale

    q_scale = HEAD_DIM ** -0.5 * LOG2E
    ones_col = jnp.where(lane == 0, 1.0, 0.0).astype(BF16)
    for h in range(N_HEADS):
        sl = slice(h * LANES, (h + 1) * LANES)
        qo_ref[:, sl] = norm_rope(q_ref[:, sl].astype(F32), qg_ref[...], q_scale).astype(BF16)
        ko_ref[:, sl] = norm_rope(k_ref[:, sl].astype(F32), kg_ref[...], 1.0).astype(BF16)
        vo_ref[:, 2 * h * LANES:(2 * h + 1) * LANES] = v_ref[:, sl]
        vo_ref[:, (2 * h + 1) * LANES:(2 * h + 2) * LANES] = ones_col


def qk_prep(proj, pos_col, freq_row, qg_row, kg_row, *, tr=512):
    t = proj.shape[0]
    aw = ATTN_WIDTH
    base = 3 * CONV_WIDTH // aw
    return pl.pallas_call(
        _qk_prep_kernel,
        grid=(t // tr,),
        in_specs=[pl.BlockSpec((tr, aw), lambda i: (i, base)),
                  pl.BlockSpec((tr, aw), lambda i: (i, base + 1)),
                  pl.BlockSpec((tr, aw), lambda i: (i, base + 2)),
                  pl.BlockSpec((tr, 1), lambda i: (i, 0)),
                  pl.BlockSpec((1, LANES), lambda i: (0, 0)),
                  pl.BlockSpec((1, LANES), lambda i: (0, 0)),
                  pl.BlockSpec((1, LANES), lambda i: (0, 0))],
        out_specs=[pl.BlockSpec((tr, aw), lambda i: (i, 0)),
                   pl.BlockSpec((tr, aw), lambda i: (i, 0)),
                   pl.BlockSpec((tr, 2 * aw), lambda i: (i, 0))],
        out_shape=[jax.ShapeDtypeStruct((t, aw), BF16),
                   jax.ShapeDtypeStruct((t, aw), BF16),
                   jax.ShapeDtypeStruct((t, 2 * aw), BF16)],
        compiler_params=_params("arbitrary"),
        name="qk_prep",
    )(proj, proj, proj, pos_col, freq_row, qg_row, kg_row)


def _diff_attn_kernel(q_ref, k_ref, v_ref, lq1_ref, lk1_ref, lq2_ref, lk2_ref, sg_ref, o_ref,
                      q12_ref, m_ref, acc_ref, *, tq, lam_init):
    qi = pl.program_id(2)
    q = q_ref[...]
    lane = lax.broadcasted_iota(jnp.int32, q.shape, 1)
    zero = jnp.zeros_like(q)
    q12_ref[0:tq, :] = jnp.where(lane < HEAD_DIM, q, zero)
    q12_ref[tq:2 * tq, :] = jnp.where(lane < HEAD_DIM, zero, q)
    m_ref[...] = jnp.full(m_ref.shape, -jnp.inf, F32)
    acc_ref[...] = jnp.zeros(acc_ref.shape, F32)
    tk = 2 * tq

    def step(jj, masked):
        start = pl.multiple_of(jj * tk, tk)
        k = k_ref[pl.ds(start, tk), :]
        v = v_ref[pl.ds(start, tk), :]
        for half in range(2):
            rows = slice(half * tq, (half + 1) * tq)
            s = lax.dot_general(q12_ref[rows, :], k, NT_DIMS, preferred_element_type=F32)
            if masked:
                q_pos = lax.broadcasted_iota(jnp.int32, s.shape, 0) + qi * tq
                k_pos = lax.broadcasted_iota(jnp.int32, s.shape, 1) + jj * tk
                s = jnp.where(k_pos <= q_pos, s, -jnp.inf)
            m_prev = m_ref[rows, :]
            m_new = jnp.maximum(m_prev, jnp.max(s, axis=-1, keepdims=True))
            alpha = jnp.exp2(m_prev - m_new)
            p = jnp.exp2(s - jnp.concatenate([m_new] * (tk // LANES), axis=1))
            pv = jnp.dot(p.astype(BF16), v, preferred_element_type=F32)
            acc_ref[rows, :] = jnp.concatenate([alpha, alpha], axis=1) * acc_ref[rows, :] + pv
            m_ref[rows, :] = m_new

    def body(jj, carry):
        step(jj, False)
        return carry

    lax.fori_loop(0, qi // 2, body, 0)
    step(qi // 2, True)

    acc = acc_ref[...]
    o12 = acc[:, :V_DIM] / acc[:, V_DIM:V_DIM + 1]
    lam = (jnp.exp(jnp.sum(lq1_ref[...] * lk1_ref[...], axis=-1, keepdims=True))
           - jnp.exp(jnp.sum(lq2_ref[...] * lk2_ref[...], axis=-1, keepdims=True)) + lam_init)
    o = o12[0:tq] - lam * o12[tq:2 * tq]
    ms = jnp.mean(o * o, axis=-1, keepdims=True)
    o = o * lax.rsqrt(ms + SUBLN_EPS) * sg_ref[...] * (1.0 - lam_init)
    o_ref[...] = o.astype(o_ref.dtype)


def diff_attn(qh, kh, vp, lq1, lk1, lq2, lk2, subln_g, *, batch, seq, lam_init, tq=512):
    t = qh.shape[0]
    nq = seq // tq
    small = pl.BlockSpec((1, HEAD_DIM), lambda b, h, i: (0, 0))
    return pl.pallas_call(
        functools.partial(_diff_attn_kernel, tq=tq, lam_init=lam_init),
        grid=(batch, N_HEADS, nq),
        in_specs=[pl.BlockSpec((tq, LANES), lambda b, h, i: (b * nq + i, h)),
                  pl.BlockSpec((seq, LANES), lambda b, h, i: (b, h)),
                  pl.BlockSpec((seq, 2 * LANES), lambda b, h, i: (b, h)),
                  small, small, small, small,
                  pl.BlockSpec((1, V_DIM), lambda b, h, i: (0, 0))],
        out_specs=pl.BlockSpec((tq, V_DIM), lambda b, h, i: (b * nq + i, h)),
        out_shape=jax.ShapeDtypeStruct((t, ATTN_WIDTH), BF16),
        scratch_shapes=[pltpu.VMEM((2 * tq, LANES), BF16),
                        pltpu.VMEM((2 * tq, LANES), F32),
                        pltpu.VMEM((2 * tq, 2 * LANES), F32)],
        compiler_params=_params("arbitrary", "arbitrary", "arbitrary"),
        name="diff_attn",
    )(qh, kh, vp, lq1, lk1, lq2, lk2, subln_g)


def _out_proj_kernel(yc_ref, ya_ref, wc_ref, wa_ref, x_ref, g1_ref, g_ref, sh_ref, sc_ref, x1_ref, h2_ref):
    mix = jnp.dot(yc_ref[...], wc_ref[...], preferred_element_type=F32)
    mix += jnp.dot(ya_ref[...], wa_ref[...], preferred_element_type=F32)
    x1 = x_ref[...] + g1_ref[0] * mix
    x1_ref[...] = x1
    h2_ref[...] = _modulated_norm(x1, g_ref[...], sh_ref[0], sc_ref[0]).T.astype(BF16)


def out_proj(yc, ya, w_bf, x2, gate1, g, shift, scale, *, seq, tm=512):
    t, d = x2.shape
    cw = yc.shape[1]
    per_batch = seq // tm
    mod_spec = pl.BlockSpec((1, 1, d), lambda i: (i // per_batch, 0, 0))
    return pl.pallas_call(
        _out_proj_kernel,
        grid=(t // tm,),
        in_specs=[pl.BlockSpec((tm, cw), lambda i: (i, 0)),
                  pl.BlockSpec((tm, cw), lambda i: (i, 0)),
                  pl.BlockSpec((cw, d), lambda i: (0, 0)),
                  pl.BlockSpec((cw, d), lambda i: (1, 0)),
                  pl.BlockSpec((tm, d), lambda i: (i, 0)),
                  mod_spec,
                  pl.BlockSpec((1, d), lambda i: (0, 0)),
                  mod_spec, mod_spec],
        out_specs=[pl.BlockSpec((tm, d), lambda i: (i, 0)),
                   pl.BlockSpec((d, tm), lambda i: (0, i))],
        out_shape=[jax.ShapeDtypeStruct((t, d), F32),
                   jax.ShapeDtypeStruct((d, t), BF16)],
        compiler_params=_params("arbitrary"),
        name="out_proj",
    )(yc, ya, w_bf, w_bf, x2, gate1, g, shift, scale)


SUB = 8


def _peer_select_kernel(h2_ref, wq_ref, keys_ref, s1_ref, s2_ref, e2_ref, r1_ref, tau_ref,
                        qt_ref, s_ref, top_ref, cnt_ref, *, ts):
    qt_ref[...] = jnp.dot(wq_ref[...], h2_ref[...], preferred_element_type=F32).astype(BF16)
    n_groups = ts // LANES
    neg_inf = -jnp.inf

    def scores_and_top(hc, carry):
        row0 = pl.multiple_of(hc * N_KEYS, N_KEYS)
        s = jnp.dot(keys_ref[hc % 2], qt_ref[pl.ds(row0, N_KEYS), :], preferred_element_type=F32)
        s_ref[hc] = s
        for g in range(n_groups):
            lanes = slice(g * LANES, (g + 1) * LANES)
            x = s[:, lanes]
            for k in range(PEER_TOPK):
                m = jnp.max(x, axis=0, keepdims=True)
                eq = x == m
                top_ref[hc, k:k + 1, lanes] = m
                cnt_ref[hc, k:k + 1, lanes] = jnp.sum(jnp.where(eq, 1.0, 0.0), axis=0, keepdims=True)
                x = jnp.where(eq, neg_inf, x)
        return carry

    lax.fori_loop(0, 2 * PEER_HEADS, scores_and_top, 0)

    def threshold(h, carry):
        for g in range(n_groups):
            lanes = slice(g * LANES, (g + 1) * LANES)
            a = top_ref[2 * h, :, lanes]
            ma = cnt_ref[2 * h, :, lanes]
            b = top_ref[2 * h + 1, :, lanes]
            mb = cnt_ref[2 * h + 1, :, lanes]
            cand = [a[k:k + 1] + b[0:SUB] for k in range(SUB)]
            mult = [ma[k:k + 1] * mb[0:SUB] for k in range(SUB)]
            cand += [a[0:1] + b[SUB:2 * SUB], a[SUB:2 * SUB] + b[0:1]]
            mult += [ma[0:1] * mb[SUB:2 * SUB], ma[SUB:2 * SUB] * mb[0:1]]
            c0 = jnp.concatenate(cand, axis=0)
            mu = jnp.concatenate(mult, axis=0)
            c = c0
            run = jnp.zeros((1, LANES), F32)
            tau = jnp.zeros((1, LANES), F32)
            for _ in range(PEER_TOPK):
                m = jnp.max(c, axis=0, keepdims=True)
                eq = c == m
                tau = jnp.where(run < PEER_TOPK, m, tau)
                run = run + jnp.sum(jnp.where(eq, mu, 0.0), axis=0, keepdims=True)
                c = jnp.where(eq, neg_inf, c)
            top = a[0:1] + b[0:1]
            z = jnp.sum(jnp.where(c0 >= tau, mu * jnp.exp(c0 - top), 0.0), axis=0, keepdims=True)
            s1 = s_ref[2 * h, :, lanes]
            s2 = s_ref[2 * h + 1, :, lanes]
            tau_ref[h, :, lanes] = tau
            s1_ref[h, :, lanes] = s1
            s2_ref[h, :, lanes] = s2
            e2_ref[h, :, lanes] = jnp.exp(s2 - b[0:1])
            r1_ref[h, :, lanes] = jnp.exp(s1 - a[0:1]) / z
        return carry

    lax.fori_loop(0, PEER_HEADS, threshold, 0)


def peer_select(h2t, wq_t, keys, *, ts=256):
    d, t = h2t.shape
    nq = wq_t.shape[0]
    hp = PEER_HEADS
    slab = pl.BlockSpec((hp, N_KEYS, ts), lambda i: (0, 0, i))
    slab_shape = jax.ShapeDtypeStruct((hp, N_KEYS, t), F32)
    return pl.pallas_call(
        functools.partial(_peer_select_kernel, ts=ts),
        grid=(t // ts,),
        in_specs=[pl.BlockSpec((d, ts), lambda i: (0, i)),
                  pl.BlockSpec((nq, d), lambda i: (0, 0)),
                  pl.BlockSpec((2, N_KEYS, N_KEYS), lambda i: (0, 0, 0))],
        out_specs=[slab, slab, slab, slab,
                   pl.BlockSpec((hp, 1, ts), lambda i: (0, 0, i))],
        out_shape=[slab_shape, slab_shape, slab_shape, slab_shape,
                   jax.ShapeDtypeStruct((hp, 1, t), F32)],
        scratch_shapes=[pltpu.VMEM((nq, ts), BF16),
                        pltpu.VMEM((2 * hp, N_KEYS, ts), F32),
                        pltpu.VMEM((2 * hp, PEER_TOPK, ts), F32),
                        pltpu.VMEM((2 * hp, PEER_TOPK, ts), F32)],
        compiler_params=_params("arbitrary"),
        name="peer_select",
    )(h2t, wq_t, keys)


ROWS_PER_STEP = SUB
EXPERTS_PER_STEP = ROWS_PER_STEP * N_KEYS


PIPELINE_LAG = 2


ROWS_PER_ITER = 2
ITERS_PER_STEP = ROWS_PER_STEP // ROWS_PER_ITER
J_SPLIT = 4


def _ffn_stages(h2t_ref, u_ref, vt_ref, s1_ref, r1_ref, s2_ref, e2_ref, tau_ref, acc_ref,
                a_out, a_in, w_out, w_in):
    ec, tb = a_out.shape
    d = acc_ref.shape[0]
    a_rows = ec // ITERS_PER_STEP
    d_rows = d // ITERS_PER_STEP
    j_rows = N_KEYS // J_SPLIT

    def body(it, carry):
        for g in range(tb // LANES):
            lanes = slice(g * LANES, (g + 1) * LANES)
            for jh in range(J_SPLIT):
                js = slice(jh * j_rows, (jh + 1) * j_rows)
                gsum = [jnp.zeros((j_rows, LANES), F32) for _ in range(ROWS_PER_ITER)]
                for h in range(PEER_HEADS):
                    s2 = s2_ref[h, js, lanes]
                    e2 = e2_ref[h, js, lanes]
                    tau = tau_ref[h, :, lanes]
                    for kk in range(ROWS_PER_ITER):
                        k = it * ROWS_PER_ITER + kk
                        sel = jnp.where(s2 + s1_ref[k, h:h + 1, lanes] >= tau, e2, 0.0)
                        gsum[kk] = gsum[kk] + sel * r1_ref[k, h:h + 1, lanes]
                for kk in range(ROWS_PER_ITER):
                    r0 = pl.multiple_of((it * ROWS_PER_ITER + kk) * N_KEYS + jh * j_rows, j_rows)
                    a = a_in[pl.ds(r0, j_rows), lanes]
                    gelu = 0.5 * a * (1.0 + lax.erf(a * (2.0 ** -0.5)))
                    w_out[pl.ds(r0, j_rows), lanes] = (gelu * gsum[kk]).astype(BF16)

        ar = pl.ds(pl.multiple_of(it * a_rows, a_rows), a_rows)
        a_out[ar, :] = jnp.dot(u_ref[ar, :], h2t_ref[...], preferred_element_type=F32)

        dr = pl.ds(pl.multiple_of(it * d_rows, d_rows), d_rows)
        acc_ref[dr, :] += jnp.dot(vt_ref[dr, :], w_in[...], preferred_element_type=F32)
        return carry

    lax.fori_loop(0, ITERS_PER_STEP, body, 0)


def _peer_ffn_kernel(h2_ref, u_ref, vt_ref, s1_ref, r1_ref, s2_ref, e2_ref, tau_ref, x1_ref, g2_ref, o_ref,
                     acc_ref, a0_ref, a1_ref, w0_ref, w1_ref, *, chunks):
    g = pl.program_id(0)
    c_out = (g + chunks - PIPELINE_LAG) % chunks
    live = g >= PIPELINE_LAG

    @pl.when(g == 0)
    def _():
        a0_ref[...] = jnp.zeros(a0_ref.shape, F32)
        a1_ref[...] = jnp.zeros(a1_ref.shape, F32)
        w0_ref[...] = jnp.zeros(w0_ref.shape, BF16)
        w1_ref[...] = jnp.zeros(w1_ref.shape, BF16)

    @pl.when(jnp.logical_or(g == 0, c_out == 0))
    def _():
        acc_ref[...] = jnp.zeros(acc_ref.shape, F32)

    operands = (h2_ref, u_ref, vt_ref, s1_ref, r1_ref, s2_ref, e2_ref, tau_ref, acc_ref)

    @pl.when(g % 2 == 0)
    def _():
        _ffn_stages(*operands, a0_ref, a1_ref, w0_ref, w1_ref)

    @pl.when(g % 2 == 1)
    def _():
        _ffn_stages(*operands, a1_ref, a0_ref, w1_ref, w0_ref)

    @pl.when(jnp.logical_and(live, c_out == chunks - 1))
    def _():
        o_ref[...] = x1_ref[...] + g2_ref[0] * acc_ref[...].T


def peer_ffn(h2t, u_bf, vt_bf, s1t, s2, e2, r1t, tau, x1, gate2, *, seq, tb=512):
    d, t = h2t.shape
    n_exp = u_bf.shape[0]
    hp = PEER_HEADS
    ec = EXPERTS_PER_STEP
    chunks = n_exp // ec
    per_batch = seq // tb
    n_steps = (t // tb) * chunks

    def lagged(lag):
        return lambda g: jnp.clip(g - lag, 0, n_steps - 1)

    ga, gb, gc = lagged(0), lagged(1), lagged(PIPELINE_LAG)
    rows = pl.BlockSpec((ROWS_PER_STEP, hp, tb), lambda g: (gb(g) % chunks, 0, gb(g) // chunks))
    slab = pl.BlockSpec((hp, N_KEYS, tb), lambda g: (0, 0, gb(g) // chunks))
    return pl.pallas_call(
        functools.partial(_peer_ffn_kernel, chunks=chunks),
        grid=(n_steps + PIPELINE_LAG,),
        in_specs=[pl.BlockSpec((d, tb), lambda g: (0, ga(g) // chunks)),
                  pl.BlockSpec((ec, d), lambda g: (ga(g) % chunks, 0)),
                  pl.BlockSpec((d, ec), lambda g: (0, gc(g) % chunks)),
                  rows, rows, slab, slab,
                  pl.BlockSpec((hp, 1, tb), lambda g: (0, 0, gb(g) // chunks)),
                  pl.BlockSpec((tb, d), lambda g: (gc(g) // chunks, 0), pipeline_mode=pl.Buffered(1)),
                  pl.BlockSpec((1, 1, d), lambda g: ((gc(g) // chunks) // per_batch, 0, 0))],
        out_specs=pl.BlockSpec((tb, d), lambda g: (gc(g) // chunks, 0)),
        out_shape=jax.ShapeDtypeStruct((t, d), F32),
        scratch_shapes=[pltpu.VMEM((d, tb), F32),
                        pltpu.VMEM((ec, tb), F32),
                        pltpu.VMEM((ec, tb), F32),
                        pltpu.VMEM((ec, tb), BF16),
                        pltpu.VMEM((ec, tb), BF16)],
        compiler_params=_params("arbitrary", flags=INTERLEAVE_STREAMS),
        name="peer_ffn",
    )(h2t, u_bf, vt_bf, s1t, r1t, s2, e2, tau, x1, gate2)


def _layer(x2, c_pad, pos_col, w_ada, b_ada, norm1_g, w_in, conv_w, q_norm_g, k_norm_g,
           lambda_q1, lambda_k1, lambda_q2, lambda_k2, subln_g, w_out, norm2_g,
           w_peer_q, sub_keys1, sub_keys2, expert_u, expert_v, *, batch, seq, layer_index):
    d = x2.shape[1]
    lam_init = 0.8 - 0.6 * math.exp(-0.3 * layer_index)

    mod = ada_mod(c_pad, w_ada, b_ada.reshape(1, -1))[:batch]
    shift1, scale1, gate1, shift2, scale2, gate2 = [m.reshape(batch, 1, d) for m in jnp.split(mod, 6, axis=-1)]

    proj = in_proj(x2, norm1_g.reshape(1, d), shift1, scale1, w_in.astype(BF16), seq=seq)
    y_conv = short_conv(proj, conv_w, seq=seq)

    inv_freq = ROPE_THETA ** (-jnp.arange(0, HEAD_DIM, 2, dtype=F32) / HEAD_DIM)
    freq_row = jnp.tile(inv_freq, LANES // inv_freq.shape[0]).reshape(1, LANES)
    qg_row = jnp.tile(q_norm_g, LANES // HEAD_DIM).reshape(1, LANES)
    kg_row = jnp.tile(k_norm_g, LANES // HEAD_DIM).reshape(1, LANES)
    qh, kh, vp = qk_prep(proj, pos_col, freq_row, qg_row, kg_row)
    y_attn = diff_attn(qh, kh, vp, lambda_q1.reshape(1, -1), lambda_k1.reshape(1, -1),
                       lambda_q2.reshape(1, -1), lambda_k2.reshape(1, -1), subln_g.reshape(1, -1),
                       batch=batch, seq=seq, lam_init=lam_init)

    x1, h2t = out_proj(y_conv, y_attn, w_out.astype(BF16), x2, gate1, norm2_g.reshape(1, d),
                       shift2, scale2, seq=seq)

    keys = jnp.stack([sub_keys1, sub_keys2]).astype(BF16)
    s1, s2, e2, r1, tau = peer_select(h2t, w_peer_q.T.astype(BF16), keys)
    s1t = jnp.transpose(s1, (1, 0, 2))
    r1t = jnp.transpose(r1, (1, 0, 2))
    return peer_ffn(h2t, expert_u.astype(BF16), expert_v.T.astype(BF16), s1t, s2, e2, r1t, tau, x1, gate2,
                    seq=seq)


def kernel(x, c, positions, w_ada, b_ada, norm1_g, w_in, conv_w, q_norm_g, k_norm_g, lambda_q1, lambda_k1,
           lambda_q2, lambda_k2, subln_g, w_out, norm2_g, w_peer_q, sub_keys1, sub_keys2, expert_u, expert_v):
    batch, seq, d = x.shape
    depth = w_ada.shape[0]
    x2 = x.reshape(batch * seq, d)
    c_pad = jnp.pad(c, ((0, SUB - batch), (0, 0)))
    pos_col = positions.astype(F32).reshape(batch * seq, 1)
    for l in range(depth):
        x2 = _layer(x2, c_pad, pos_col, w_ada[l], b_ada[l], norm1_g[l], w_in[l], conv_w[l], q_norm_g[l],
                    k_norm_g[l], lambda_q1[l], lambda_k1[l], lambda_q2[l], lambda_k2[l], subln_g[l],
                    w_out[l], norm2_g[l], w_peer_q[l], sub_keys1[l], sub_keys2[l], expert_u[l], expert_v[l],
                    batch=batch, seq=seq, layer_index=l)
    return x2.reshape(batch, seq, d)
```
